```python
import jax, jax.numpy as jnp
from jax import lax
import numpy as np

D_MODEL = 1024
BATCH = 8
SEQ = 4096
DEPTH = 2

CHUNK = 64
CONV_W = D_MODEL
CONV_A_K = 3
LRU_W = D_MODEL
LRU_HEADS = 8
LRU_BW = LRU_W // LRU_HEADS
CONV_B_K = 4
LRU_C = 8.0
D_FF = 4 * D_MODEL
N_MOD = 6
N_IN = 3 * CONV_W + 2 * LRU_W + 2 * D_MODEL
EPS = 1e-6
SPLITS = (CONV_W, 2 * CONV_W, 3 * CONV_W, 3 * CONV_W + LRU_W,
          3 * CONV_W + 2 * LRU_W, 3 * CONV_W + 2 * LRU_W + D_MODEL)

kernel_name = "hybrid_shortconv_rglru_sandwich_adaln"


def rms_norm(x, g):
    xf = x.astype(jnp.float32)
    y = xf * lax.rsqrt(jnp.mean(xf * xf, axis=-1, keepdims=True) + EPS)
    return (y * g.astype(jnp.float32)).astype(x.dtype)


def causal_dwconv(x, w, b):
    k_w = w.shape[0]
    s = x.shape[1]
    xp = jnp.pad(x, ((0, 0), (k_w - 1, 0), (0, 0)))
    y = b
    for k in range(k_w):
        y = y + xp[:, k:k + s, :] * w[k]
    return y


def block_diag_linear(x, w, b):
    bsz, s, _ = x.shape
    xh = x.reshape(bsz, s, LRU_HEADS, LRU_BW)
    y = jnp.einsum("bshi,hij->bshj", xh, w).reshape(bsz, s, LRU_W)
    return y + b


def _lin_combine(left, right):
    a1, b1 = left
    a2, b2 = right
    return a1 * a2, a2 * b1 + b2


def chunked_linear_scan(a, b):
    bsz, s, w = a.shape
    n = s // CHUNK
    a_c = a.reshape(bsz, n, CHUNK, w)
    b_c = b.reshape(bsz, n, CHUNK, w)
    a_cum, h_loc = lax.associative_scan(_lin_combine, (a_c, b_c), axis=2)

    def step(h_prev, inp):
        a_k, h_k = inp
        h_full = h_k + a_k * h_prev[:, None, :]
        return h_full[:, -1, :], h_full

    h0 = jnp.zeros((bsz, w), a.dtype)
    _, hs = lax.scan(step, h0, (jnp.swapaxes(a_cum, 0, 1), jnp.swapaxes(h_loc, 0, 1)))
    return jnp.swapaxes(hs, 0, 1).reshape(bsz, s, w)


def rg_lru(x, w_r, b_r, w_i, b_i, lam):
    xf = x.astype(jnp.float32)
    r = jax.nn.sigmoid(block_diag_linear(xf, w_r.astype(jnp.float32), b_r.astype(jnp.float32)))
    i = jax.nn.sigmoid(block_diag_linear(xf, w_i.astype(jnp.float32), b_i.astype(jnp.float32)))
    log_a = -LRU_C * r * jax.nn.softplus(-lam.astype(jnp.float32))
    a = jnp.exp(log_a)
    mult = jnp.sqrt(-jnp.expm1(2.0 * log_a))
    h = chunked_linear_scan(a, mult * (i * xf))
    return h.astype(x.dtype)


def hybrid_mixer(h, w_in, conv_a_w, conv_a_b, w_a_out, conv_b_w, conv_b_b,
                 w_gate_r, b_gate_r, w_gate_i, b_gate_i, lru_lambda, w_b_out, w_o):
    proj = h @ w_in
    b_a, c_a, v_a, x_b, g_b, u_a, u_b = jnp.split(proj, SPLITS, axis=-1)
    y_a = b_a * causal_dwconv(c_a * v_a, conv_a_w, conv_a_b)
    x_b = causal_dwconv(x_b, conv_b_w, conv_b_b)
    y_b = rg_lru(x_b, w_gate_r, b_gate_r, w_gate_i, b_gate_i, lru_lambda) * jax.nn.gelu(g_b)
    m = jax.nn.sigmoid(u_a) * (y_a @ w_a_out) + jax.nn.sigmoid(u_b) * (y_b @ w_b_out)
    return m @ w_o


def sq_relu_mlp(h, w_up, w_down):
    return jnp.square(jax.nn.relu(h @ w_up)) @ w_down


def setup_inputs(seed: int = 0) -> dict:
    key = jax.random.key(seed)
    ks = jax.random.split(key, 24)
    L, D = DEPTH, D_MODEL
    f32 = jnp.float32

    def nrm(k, shape, scale):
        return jax.random.normal(k, shape, f32) * scale

    u = jax.random.uniform(ks[15], (L, LRU_W), f32, 0.9, 0.999)
    return {
        "x": nrm(ks[0], (BATCH, SEQ, D), 1.0),
        "c": nrm(ks[1], (BATCH, D), 1.0),
        "w_mod": nrm(ks[2], (L, D, N_MOD * D), 0.5 * D ** -0.5),
        "b_mod": nrm(ks[3], (L, N_MOD * D), 0.02),
        "g_pre_mix": 1.0 + nrm(ks[4], (L, D), 0.02),
        "g_post_mix": 1.0 + nrm(ks[5], (L, D), 0.02),
        "w_in": nrm(ks[6], (L, D, N_IN), D ** -0.5),
        "conv_a_w": nrm(ks[7], (L, CONV_A_K, CONV_W), CONV_A_K ** -0.5),
        "conv_a_b": nrm(ks[8], (L, CONV_W), 0.02),
        "w_a_out": nrm(ks[9], (L, CONV_W, D), CONV_W ** -0.5),
        "conv_b_w": nrm(ks[10], (L, CONV_B_K, LRU_W), CONV_B_K ** -0.5),
        "conv_b_b": nrm(ks[11], (L, LRU_W), 0.02),
        "w_gate_r": nrm(ks[12], (L, LRU_HEADS, LRU_BW, LRU_BW), LRU_BW ** -0.5),
        "b_gate_r": nrm(ks[13], (L, LRU_W), 0.02),
        "w_gate_i": nrm(ks[14], (L, LRU_HEADS, LRU_BW, LRU_BW), LRU_BW ** -0.5),
        "b_gate_i": nrm(ks[16], (L, LRU_W), 0.02),
        "lru_lambda": jnp.log(u) - jnp.log1p(-u),
        "w_b_out": nrm(ks[17], (L, LRU_W, D), LRU_W ** -0.5),
        "w_o": nrm(ks[18], (L, D, D), D ** -0.5),
        "g_pre_mlp": 1.0 + nrm(ks[19], (L, D), 0.02),
        "g_post_mlp": 1.0 + nrm(ks[20], (L, D), 0.02),
        "w_mlp_up": nrm(ks[21], (L, D, D_FF), D ** -0.5),
        "w_mlp_down": nrm(ks[22], (L, D_FF, D), D_FF ** -0.5),
    }


def reference(x, c, w_mod, b_mod, g_pre_mix, g_post_mix, w_in, conv_a_w, conv_a_b,
              w_a_out, conv_b_w, conv_b_b, w_gate_r, b_gate_r, w_gate_i, b_gate_i,
              lru_lambda, w_b_out, w_o, g_pre_mlp, g_post_mlp, w_mlp_up, w_mlp_down):
    c_act = jax.nn.silu(c)
    for l in range(DEPTH):
        mod = (c_act @ w_mod[l] + b_mod[l])[:, None, :]
        sh_m, sc_m, gt_m, sh_f, sc_f, gt_f = jnp.split(mod, N_MOD, axis=-1)
        h = rms_norm(x, g_pre_mix[l]) * (1.0 + sc_m) + sh_m
        y = hybrid_mixer(h, w_in[l], conv_a_w[l], conv_a_b[l], w_a_out[l],
                         conv_b_w[l], conv_b_b[l], w_gate_r[l], b_gate_r[l],
                         w_gate_i[l], b_gate_i[l], lru_lambda[l], w_b_out[l], w_o[l])
        x = x + gt_m * rms_norm(y, g_post_mix[l])
        h = rms_norm(x, g_pre_mlp[l]) * (1.0 + sc_f) + sh_f
        y = sq_relu_mlp(h, w_mlp_up[l], w_mlp_down[l])
        x = x + gt_f * rms_norm(y, g_post_mlp[l])
    return x
```

```python
import functools

import jax
import jax.numpy as jnp
from jax import lax
from jax.experimental import pallas as pl
from jax.experimental.pallas import tpu as pltpu

D = 1024
BATCH = 8
SEQ = 4096
DEPTH = 2
D_FF = 4 * D
N_MOD = 6
N_IN = 7 * D
HEADS = 8
HEAD_W = D // HEADS
CONV_A_K = 3
CONV_B_K = 4
LRU_C = 8.0
EPS = 1e-6

TT = 64
M = TT * BATCH
HALO_A = (CONV_A_K - 1) * BATCH
HALO_B = (CONV_B_K - 1) * BATCH
RC = 32
RC_GATE = 16
SCAN_STEPS = 4

C_B, C_C, C_V, C_X, C_G, C_UA, C_UB = (k * D for k in range(7))

VMEM_LIMIT = 60 * 1024 * 1024

f32 = jnp.float32
bf16 = jnp.bfloat16


def _sigmoid(x):
    return 0.5 * jnp.tanh(0.5 * x) + 0.5


def _gelu_tanh(x):
    return 0.5 * x * (1.0 + jnp.tanh(0.7978845608028654 * (x + 0.044715 * (x * x * x))))


def _softplus(x):
    return jnp.maximum(x, 0.0) + jnp.log1p(jnp.exp(-jnp.abs(x)))


def _for_chunks(n_rows, rc, body):
    def step(k, carry):
        body(pl.multiple_of(k * rc, rc))
        return carry
    lax.fori_loop(0, n_rows // rc, step, 0)


def _per_batch(v, p):
    rows = v.shape[0]
    return (v.reshape(rows // BATCH, BATCH, D) * p[None]).reshape(rows, D)


def _rms_scale(v):
    ms = jnp.mean(v * v, axis=-1, keepdims=True)
    return v * lax.rsqrt(ms + EPS)


def _mod_kernel(c_ref, w_ref, b_ref, o_ref):
    c = c_ref[...]
    c_act = (c * _sigmoid(c)).astype(bf16)
    o_ref[0, 0] = jnp.dot(c_act, w_ref[0].astype(bf16), preferred_element_type=f32) + b_ref[0, 0]


def _modulation(c, w_mod, b_mod):
    return pl.pallas_call(
        _mod_kernel,
        grid=(DEPTH, N_MOD),
        in_specs=[
            pl.BlockSpec((BATCH, D), lambda l, j: (0, 0)),
            pl.BlockSpec((1, D, D), lambda l, j: (l, 0, j)),
            pl.BlockSpec((1, 1, 1, D), lambda l, j: (l, j, 0, 0)),
        ],
        out_specs=pl.BlockSpec((1, 1, BATCH, D), lambda l, j: (l, j, 0, 0)),
        out_shape=jax.ShapeDtypeStruct((DEPTH, N_MOD, BATCH, D), f32),
        compiler_params=pltpu.CompilerParams(dimension_semantics=("arbitrary", "arbitrary")),
        name="adaln_modulation",
    )(c, w_mod, b_mod.reshape(DEPTH, N_MOD, 1, D))


def _mixer_kernel(x_ref, mod_ref, gpre_ref, gpost_ref, w_in_ref, caw_ref, cab_ref, w_a_ref,
                  cbw_ref, cbb_ref, wg_ref, bgr_ref, bgi_ref, lam_ref, w_b_ref, w_o_ref,
                  o_ref, p_ref, cv_ref, xc_ref, act_a, act_b, h_ref):
    i = pl.program_id(0)

    @pl.when(i == 0)
    def _():
        cv_ref[pl.ds(0, HALO_A), :] = jnp.zeros((HALO_A, D), f32)
        p_ref[pl.ds(0, HALO_B), C_X:C_X + D] = jnp.zeros((HALO_B, D), f32)
        h_ref[...] = jnp.zeros((BATCH, D), f32)

    scale = gpre_ref[...] * (1.0 + mod_ref[1])
    shift = mod_ref[0]

    def pre(r0):
        y = _rms_scale(x_ref[pl.ds(r0, RC), :])
        rows = y.reshape(RC // BATCH, BATCH, D) * scale[None] + shift[None]
        act_a[pl.ds(r0, RC), :] = rows.reshape(RC, D).astype(bf16)
    _for_chunks(M, RC, pre)

    p_ref[pl.ds(HALO_B, M), :] = jnp.dot(act_a[...], w_in_ref[...], preferred_element_type=f32)

    def cv(r0):
        cv_ref[pl.ds(HALO_A + r0, RC), :] = (p_ref[pl.ds(HALO_B + r0, RC), C_C:C_C + D]
                                             * p_ref[pl.ds(HALO_B + r0, RC), C_V:C_V + D])
    _for_chunks(M, RC, cv)

    def convs(r0):
        acc = cab_ref[...] + caw_ref[0:1, :] * cv_ref[pl.ds(r0, RC), :]
        for k in range(1, CONV_A_K):
            acc = acc + caw_ref[k:k + 1, :] * cv_ref[pl.ds(r0 + k * BATCH, RC), :]
        y_a = p_ref[pl.ds(HALO_B + r0, RC), C_B:C_B + D] * acc
        act_a[pl.ds(r0, RC), :] = y_a.astype(bf16)
        xc = cbb_ref[...] + cbw_ref[0:1, :] * p_ref[pl.ds(r0, RC), C_X:C_X + D]
        for k in range(1, CONV_B_K):
            xc = xc + cbw_ref[k:k + 1, :] * p_ref[pl.ds(r0 + k * BATCH, RC), C_X:C_X + D]
        xc_ref[pl.ds(r0, RC), :] = xc
        act_b[pl.ds(r0, RC), :] = xc.astype(bf16)
    _for_chunks(M, RC, convs)

    cv_ref[pl.ds(0, HALO_A), :] = cv_ref[pl.ds(M, HALO_A), :]
    p_ref[pl.ds(0, HALO_B), C_X:C_X + D] = p_ref[pl.ds(M, HALO_B), C_X:C_X + D]

    p_ref[pl.ds(HALO_B, M), C_B:C_B + D] = jnp.dot(act_a[...], w_a_ref[...],
                                                   preferred_element_type=f32)
    for hd in range(HEADS):
        lo = hd * HEAD_W
        g = jnp.dot(act_b[:, lo:lo + HEAD_W], wg_ref[hd], preferred_element_type=f32)
        p_ref[pl.ds(HALO_B, M), C_C + lo:C_C + lo + HEAD_W] = g[:, :HEAD_W]
        p_ref[pl.ds(HALO_B, M), C_V + lo:C_V + lo + HEAD_W] = g[:, HEAD_W:]

    lam = lam_ref[...]
    neg_c_softplus = -LRU_C * _softplus(-lam)

    def gates(r0):
        rows = pl.ds(HALO_B + r0, RC_GATE)
        r = _sigmoid(p_ref[rows, C_C:C_C + D] + bgr_ref[...])
        ig = _sigmoid(p_ref[rows, C_V:C_V + D] + bgi_ref[...])
        log_a = r * neg_c_softplus
        a = jnp.exp(log_a)
        mult = jnp.sqrt(-jnp.tanh(log_a) * (a * a + 1.0))
        p_ref[rows, C_C:C_C + D] = a
        p_ref[rows, C_V:C_V + D] = mult * (ig * xc_ref[pl.ds(r0, RC_GATE), :])
        p_ref[rows, C_G:C_G + D] = _gelu_tanh(p_ref[rows, C_G:C_G + D])
    _for_chunks(M, RC_GATE, gates)

    def scan(k, h):
        base = pl.multiple_of(k * (SCAN_STEPS * BATCH), SCAN_STEPS * BATCH)
        for pair in range(SCAN_STEPS // 2):
            hs = []
            for s in range(2):
                rows = pl.ds(HALO_B + base + (2 * pair + s) * BATCH, BATCH)
                h = p_ref[rows, C_C:C_C + D] * h + p_ref[rows, C_V:C_V + D]
                hs.append(h)
            r16 = base + 2 * pair * BATCH
            y_b = jnp.concatenate(hs, axis=0) * p_ref[pl.ds(HALO_B + r16, 2 * BATCH), C_G:C_G + D]
            act_b[pl.ds(r16, 2 * BATCH), :] = y_b.astype(bf16)
        return h
    h_ref[...] = lax.fori_loop(0, TT // SCAN_STEPS, scan, h_ref[...])

    p_ref[pl.ds(HALO_B, M), C_X:C_X + D] = jnp.dot(act_b[...], w_b_ref[...],
                                                   preferred_element_type=f32)

    def merge(r0):
        rows = pl.ds(HALO_B + r0, RC)
        m = (_sigmoid(p_ref[rows, C_UA:C_UA + D]) * p_ref[rows, C_B:C_B + D]
             + _sigmoid(p_ref[rows, C_UB:C_UB + D]) * p_ref[rows, C_X:C_X + D])
        act_a[pl.ds(r0, RC), :] = m.astype(bf16)
    _for_chunks(M, RC, merge)

    p_ref[pl.ds(HALO_B, M), C_B:C_B + D] = jnp.dot(act_a[...], w_o_ref[...],
                                                   preferred_element_type=f32)
    gate = mod_ref[2] * gpost_ref[...]

    def post(r0):
        y = _rms_scale(p_ref[pl.ds(HALO_B + r0, RC), C_B:C_B + D])
        o_ref[pl.ds(r0, RC), :] = x_ref[pl.ds(r0, RC), :] + _per_batch(y, gate)
    _for_chunks(M, RC, post)


def _const_spec(shape):
    nd = len(shape)
    return pl.BlockSpec(shape, lambda i: (0,) * nd, pipeline_mode=pl.Buffered(1))


def _mixer(xt, mod_l, gpre, gpost, w_in, caw, cab, w_a, cbw, cbb, wg, bgr, bgi, lam, w_b, w_o):
    n_rows = xt.shape[0]
    consts = (mod_l, gpre, gpost, w_in, caw, cab, w_a, cbw, cbb, wg, bgr, bgi, lam, w_b, w_o)
    return pl.pallas_call(
        _mixer_kernel,
        grid=(n_rows // M,),
        in_specs=[pl.BlockSpec((M, D), lambda i: (i, 0))] + [_const_spec(a.shape) for a in consts],
        out_specs=pl.BlockSpec((M, D), lambda i: (i, 0)),
        out_shape=jax.ShapeDtypeStruct((n_rows, D), f32),
        scratch_shapes=[
            pltpu.VMEM((HALO_B + M, N_IN), f32),
            pltpu.VMEM((HALO_A + M, D), f32),
            pltpu.VMEM((M, D), f32),
            pltpu.VMEM((M, D), bf16),
            pltpu.VMEM((M, D), bf16),
            pltpu.VMEM((BATCH, D), f32),
        ],
        compiler_params=pltpu.CompilerParams(dimension_semantics=("arbitrary",),
                                             vmem_limit_bytes=VMEM_LIMIT),
        name="hybrid_mixer",
    )(xt, *consts)


def _mlp_kernel(x_ref, mod_ref, gpre_ref, gpost_ref, w_up_ref, w_dn_ref, o_ref,
                act_ref, u_ref, u16_ref, y_ref):
    scale = gpre_ref[...] * (1.0 + mod_ref[4])
    shift = mod_ref[3]

    def pre(r0):
        y = _rms_scale(x_ref[pl.ds(r0, RC), :])
        rows = y.reshape(RC // BATCH, BATCH, D) * scale[None] + shift[None]
        act_ref[pl.ds(r0, RC), :] = rows.reshape(RC, D).astype(bf16)
    _for_chunks(M, RC, pre)

    u_ref[...] = jnp.dot(act_ref[...], w_up_ref[...], preferred_element_type=f32)

    def act(r0):
        u = jnp.maximum(u_ref[pl.ds(r0, RC_GATE), :], 0.0)
        u16_ref[pl.ds(r0, RC_GATE), :] = (u * u).astype(bf16)
    _for_chunks(M, RC_GATE, act)

    y_ref[...] = jnp.dot(u16_ref[...], w_dn_ref[...], preferred_element_type=f32)
    gate = mod_ref[5] * gpost_ref[...]

    def post(r0):
        y = _rms_scale(y_ref[pl.ds(r0, RC), :])
        o_ref[pl.ds(r0, RC), :] = x_ref[pl.ds(r0, RC), :] + _per_batch(y, gate)
    _for_chunks(M, RC, post)


def _mlp(xt, mod_l, gpre, gpost, w_up, w_dn):
    n_rows = xt.shape[0]
    consts = (mod_l, gpre, gpost, w_up, w_dn)
    return pl.pallas_call(
        _mlp_kernel,
        grid=(n_rows // M,),
        in_specs=[pl.BlockSpec((M, D), lambda i: (i, 0))] + [_const_spec(a.shape) for a in consts],
        out_specs=pl.BlockSpec((M, D), lambda i: (i, 0)),
        out_shape=jax.ShapeDtypeStruct((n_rows, D), f32),
        scratch_shapes=[
            pltpu.VMEM((M, D), bf16),
            pltpu.VMEM((M, D_FF), f32),
            pltpu.VMEM((M, D_FF), bf16),
            pltpu.VMEM((M, D), f32),
        ],
        compiler_params=pltpu.CompilerParams(dimension_semantics=("arbitrary",),
                                             vmem_limit_bytes=VMEM_LIMIT),
        name="sq_relu_mlp",
    )(xt, *consts)


def kernel(x, c, w_mod, b_mod, g_pre_mix, g_post_mix, w_in, conv_a_w, conv_a_b, w_a_out, conv_b_w,
           conv_b_b, w_gate_r, b_gate_r, w_gate_i, b_gate_i, lru_lambda, w_b_out, w_o, g_pre_mlp,
           g_post_mlp, w_mlp_up, w_mlp_down):
    assert x.shape == (BATCH, SEQ, D) and c.shape == (BATCH, D)
    mod = _modulation(c, w_mod, b_mod)
    xt = jnp.transpose(x, (1, 0, 2)).reshape(SEQ * BATCH, D)
    row = lambda a: a.reshape(1, D)
    for l in range(DEPTH):
        w_gates = jnp.concatenate([w_gate_r[l], w_gate_i[l]], axis=-1).astype(bf16)
        xt = _mixer(xt, mod[l], row(g_pre_mix[l]), row(g_post_mix[l]), w_in[l].astype(bf16),
                    conv_a_w[l], row(conv_a_b[l]), w_a_out[l].astype(bf16), conv_b_w[l],
                    row(conv_b_b[l]), w_gates, row(b_gate_r[l]), row(b_gate_i[l]),
                    row(lru_lambda[l]), w_b_out[l].astype(bf16), w_o[l].astype(bf16))
        xt = _mlp(xt, mod[l], row(g_pre_mlp[l]), row(g_post_mlp[l]),
                  w_mlp_up[l].astype(bf16), w_mlp_down[l].astype(bf16))
    return jnp.transpose(xt.reshape(SEQ, BATCH, D), (1, 0, 2))
```

```python
import jax
import jax.numpy as jnp
from jax import lax
from jax.experimental import pallas as pl
from jax.experimental.pallas import tpu as pltpu

D = 1024
BATCH = 8
SEQ = 4096
DEPTH = 2
D_FF = 4 * D
N_MOD = 6
N_IN = 7 * D
HEADS = 8
HEAD_W = D // HEADS
CONV_A_K = 3
CONV_B_K = 4
LRU_C = 8.0
EPS = 1e-6

TT = 64
M = TT * BATCH
HALO_A = (CONV_A_K - 1) * BATCH
HALO_B = (CONV_B_K - 1) * BATCH
RC = 32
RC_GATE = 16
FF_BLK = D

C_B, C_C, C_V, C_X, C_G, C_UA, C_UB = (k * D for k in range(7))

VMEM_LIMIT = 60 * 1024 * 1024

f32 = jnp.float32
bf16 = jnp.bfloat16


def _sigmoid(x):
    return 0.5 * jnp.tanh(0.5 * x) + 0.5


def _gelu_tanh(x):
    return 0.5 * x * (1.0 + jnp.tanh(0.7978845608028654 * (x + 0.044715 * (x * x * x))))


def _softplus(x):
    return jnp.maximum(x, 0.0) + jnp.log1p(jnp.exp(-jnp.abs(x)))


def _sqrt_nonneg(z):
    return jnp.where(z > 0.0, z * lax.rsqrt(z), 0.0)


def _chunks(n_rows, rc):
    return range(0, n_rows, rc)


def _per_batch(v, p):
    rows = v.shape[0]
    return (v.reshape(rows // BATCH, BATCH, D) * p[None]).reshape(rows, D)


def _rms_scale(v):
    ms = jnp.mean(v * v, axis=-1, keepdims=True)
    return v * lax.rsqrt(ms + EPS)


def _dot(a, b):
    return jnp.dot(a, b, preferred_element_type=f32)


def _pre_norm(x_ref, scale, shift, dst_ref):
    for r0 in _chunks(M, RC):
        y = _rms_scale(x_ref[r0:r0 + RC, :])
        rows = y.reshape(RC // BATCH, BATCH, D) * scale[None] + shift[None]
        dst_ref[r0:r0 + RC, :] = rows.reshape(RC, D).astype(bf16)


def _post_norm_residual(x_ref, y_ref, col, gate, o_ref):
    for r0 in _chunks(M, RC):
        y = _rms_scale(y_ref[r0:r0 + RC, col:col + D])
        o_ref[r0:r0 + RC, :] = x_ref[r0:r0 + RC, :] + _per_batch(y, gate)


def _mod_kernel(c_ref, w_ref, b_ref, o_ref):
    c = c_ref[...]
    c_act = (c * _sigmoid(c)).astype(bf16)
    o_ref[0, 0] = _dot(c_act, w_ref[0].astype(bf16)) + b_ref[0, 0]


def _modulation(c, w_mod, b_mod):
    return pl.pallas_call(
        _mod_kernel,
        grid=(DEPTH, N_MOD),
        in_specs=[
            pl.BlockSpec((BATCH, D), lambda l, j: (0, 0)),
            pl.BlockSpec((1, D, D), lambda l, j: (l, 0, j)),
            pl.BlockSpec((1, 1, 1, D), lambda l, j: (l, j, 0, 0)),
        ],
        out_specs=pl.BlockSpec((1, 1, BATCH, D), lambda l, j: (l, j, 0, 0)),
        out_shape=jax.ShapeDtypeStruct((DEPTH, N_MOD, BATCH, D), f32),
        compiler_params=pltpu.CompilerParams(dimension_semantics=("arbitrary", "arbitrary")),
        name="adaln_modulation",
    )(c, w_mod, b_mod.reshape(DEPTH, N_MOD, 1, D))


def _mixer_kernel(x_ref, mod_ref, gpre_ref, gpost_ref, w_in_ref, caw_ref, cab_ref, w_a_ref,
                  cbw_ref, cbb_ref, wg_ref, bgr_ref, bgi_ref, lam_ref, w_b_ref, w_o_ref,
                  o_ref, p_ref, xc_ref, act_h, act_a, act_b, cvh_ref, xh_ref, h_ref):
    i = pl.program_id(0)

    @pl.when(i == 0)
    def _():
        cvh_ref[...] = jnp.zeros((HALO_A, D), f32)
        xh_ref[...] = jnp.zeros((HALO_B, D), f32)
        h_ref[...] = jnp.zeros((BATCH, D), f32)

    _pre_norm(x_ref, gpre_ref[...] * (1.0 + mod_ref[1]), mod_ref[0], act_h)

    p_ref[:, C_X:C_X + D] = _dot(act_h[...], w_in_ref[:, C_X:C_X + D])
    p_ref[:, C_B:C_V + D] = _dot(act_h[...], w_in_ref[:, C_B:C_V + D])

    cbw = [cbw_ref[k:k + 1, :] for k in range(CONV_B_K)]
    cbb = cbb_ref[...]
    for r0 in _chunks(M, RC):
        if r0 == 0:
            win = jnp.concatenate([xh_ref[...], p_ref[0:RC, C_X:C_X + D]], axis=0)
        else:
            win = p_ref[r0 - HALO_B:r0 + RC, C_X:C_X + D]
        xc = cbb + cbw[0] * win[0:RC]
        for k in range(1, CONV_B_K):
            xc = xc + cbw[k] * win[k * BATCH:k * BATCH + RC]
        xc_ref[r0:r0 + RC, :] = xc
        act_b[r0:r0 + RC, :] = xc.astype(bf16)
    xh_ref[...] = p_ref[M - HALO_B:M, C_X:C_X + D]

    for r0 in _chunks(M, RC):
        p_ref[r0:r0 + RC, C_C:C_C + D] = (p_ref[r0:r0 + RC, C_C:C_C + D]
                                          * p_ref[r0:r0 + RC, C_V:C_V + D])
    caw = [caw_ref[k:k + 1, :] for k in range(CONV_A_K)]
    cab = cab_ref[...]
    for r0 in _chunks(M, RC):
        if r0 == 0:
            win = jnp.concatenate([cvh_ref[...], p_ref[0:RC, C_C:C_C + D]], axis=0)
        else:
            win = p_ref[r0 - HALO_A:r0 + RC, C_C:C_C + D]
        acc = cab + caw[0] * win[0:RC]
        for k in range(1, CONV_A_K):
            acc = acc + caw[k] * win[k * BATCH:k * BATCH + RC]
        act_a[r0:r0 + RC, :] = (p_ref[r0:r0 + RC, C_B:C_B + D] * acc).astype(bf16)
    cvh_ref[...] = p_ref[M - HALO_A:M, C_C:C_C + D]

    for hd in range(HEADS):
        lo = hd * HEAD_W
        g = _dot(act_b[:, lo:lo + HEAD_W], wg_ref[hd])
        p_ref[:, C_V + lo:C_V + lo + HEAD_W] = g[:, :HEAD_W]
        p_ref[:, C_X + lo:C_X + lo + HEAD_W] = g[:, HEAD_W:]

    p_ref[:, C_G:C_UB + D] = _dot(act_h[...], w_in_ref[:, C_G:C_UB + D])
    p_ref[:, C_B:C_B + D] = _dot(act_a[...], w_a_ref[...])

    neg_c_softplus = -LRU_C * _softplus(-lam_ref[...])
    bgr = bgr_ref[...]
    bgi = bgi_ref[...]
    for r0 in _chunks(M, RC_GATE):
        rows = slice(r0, r0 + RC_GATE)
        r = _sigmoid(p_ref[rows, C_V:C_V + D] + bgr)
        ig = _sigmoid(p_ref[rows, C_X:C_X + D] + bgi)
        log_a = r * neg_c_softplus
        a = jnp.exp(log_a)
        mult = _sqrt_nonneg(-jnp.tanh(log_a) * (a * a + 1.0))
        p_ref[rows, C_V:C_V + D] = a
        p_ref[rows, C_X:C_X + D] = mult * (ig * xc_ref[rows, :])
    for r0 in _chunks(M, RC):
        p_ref[r0:r0 + RC, C_G:C_G + D] = _gelu_tanh(p_ref[r0:r0 + RC, C_G:C_G + D])

    h = h_ref[...]
    for t0 in range(0, M, 2 * BATCH):
        hs = []
        for s in range(2):
            rows = slice(t0 + s * BATCH, t0 + (s + 1) * BATCH)
            h = p_ref[rows, C_V:C_V + D] * h + p_ref[rows, C_X:C_X + D]
            hs.append(h)
        y_b = jnp.concatenate(hs, axis=0) * p_ref[t0:t0 + 2 * BATCH, C_G:C_G + D]
        act_b[t0:t0 + 2 * BATCH, :] = y_b.astype(bf16)
    h_ref[...] = h

    for r0 in _chunks(M, RC):
        rows = slice(r0, r0 + RC)
        p_ref[rows, C_B:C_B + D] = _sigmoid(p_ref[rows, C_UA:C_UA + D]) * p_ref[rows, C_B:C_B + D]
        p_ref[rows, C_UB:C_UB + D] = _sigmoid(p_ref[rows, C_UB:C_UB + D])
    p_ref[:, C_C:C_C + D] = _dot(act_b[...], w_b_ref[...])
    for r0 in _chunks(M, RC):
        rows = slice(r0, r0 + RC)
        m = p_ref[rows, C_B:C_B + D] + p_ref[rows, C_UB:C_UB + D] * p_ref[rows, C_C:C_C + D]
        act_a[rows, :] = m.astype(bf16)

    p_ref[:, C_V:C_V + D] = _dot(act_a[...], w_o_ref[...])
    _post_norm_residual(x_ref, p_ref, C_V, mod_ref[2] * gpost_ref[...], o_ref)


def _const_spec(shape):
    nd = len(shape)
    return pl.BlockSpec(shape, lambda i: (0,) * nd, pipeline_mode=pl.Buffered(1))


def _mixer(xt, mod_l, gpre, gpost, w_in, caw, cab, w_a, cbw, cbb, wg, bgr, bgi, lam, w_b, w_o):
    n_rows = xt.shape[0]
    consts = (mod_l, gpre, gpost, w_in, caw, cab, w_a, cbw, cbb, wg, bgr, bgi, lam, w_b, w_o)
    return pl.pallas_call(
        _mixer_kernel,
        grid=(n_rows // M,),
        in_specs=[pl.BlockSpec((M, D), lambda i: (i, 0))] + [_const_spec(a.shape) for a in consts],
        out_specs=pl.BlockSpec((M, D), lambda i: (i, 0)),
        out_shape=jax.ShapeDtypeStruct((n_rows, D), f32),
        scratch_shapes=[
            pltpu.VMEM((M, N_IN), f32),
            pltpu.VMEM((M, D), f32),
            pltpu.VMEM((M, D), bf16),
            pltpu.VMEM((M, D), bf16),
            pltpu.VMEM((M, D), bf16),
            pltpu.VMEM((HALO_A, D), f32),
            pltpu.VMEM((HALO_B, D), f32),
            pltpu.VMEM((BATCH, D), f32),
        ],
        compiler_params=pltpu.CompilerParams(dimension_semantics=("arbitrary",),
                                             vmem_limit_bytes=VMEM_LIMIT),
        name="hybrid_mixer",
    )(xt, *consts)


def _mlp_kernel(x_ref, mod_ref, gpre_ref, gpost_ref, w_up_ref, w_dn_ref, o_ref,
                act_ref, u_ref, u16_ref, y_ref):
    _pre_norm(x_ref, gpre_ref[...] * (1.0 + mod_ref[4]), mod_ref[3], act_ref)

    y = None
    for j in range(D_FF // FF_BLK):
        cols = slice(j * FF_BLK, (j + 1) * FF_BLK)
        u_ref[:, cols] = _dot(act_ref[...], w_up_ref[:, cols])
        for r0 in _chunks(M, RC):
            u = jnp.maximum(u_ref[r0:r0 + RC, cols], 0.0)
            u16_ref[r0:r0 + RC, cols] = (u * u).astype(bf16)
        part = _dot(u16_ref[:, cols], w_dn_ref[cols, :])
        y = part if y is None else y + part
    y_ref[...] = y

    _post_norm_residual(x_ref, y_ref, 0, mod_ref[5] * gpost_ref[...], o_ref)


def _mlp(xt, mod_l, gpre, gpost, w_up, w_dn):
    n_rows = xt.shape[0]
    consts = (mod_l, gpre, gpost, w_up, w_dn)
    return pl.pallas_call(
        _mlp_kernel,
        grid=(n_rows // M,),
        in_specs=[pl.BlockSpec((M, D), lambda i: (i, 0))] + [_const_spec(a.shape) for a in consts],
        out_specs=pl.BlockSpec((M, D), lambda i: (i, 0)),
        out_shape=jax.ShapeDtypeStruct((n_rows, D), f32),
        scratch_shapes=[
            pltpu.VMEM((M, D), bf16),
            pltpu.VMEM((M, D_FF), f32),
            pltpu.VMEM((M, D_FF), bf16),
            pltpu.VMEM((M, D), f32),
        ],
        compiler_params=pltpu.CompilerParams(dimension_semantics=("arbitrary",),
                                             vmem_limit_bytes=VMEM_LIMIT),
        name="sq_relu_mlp",
    )(xt, *consts)


def kernel(x, c, w_mod, b_mod, g_pre_mix, g_post_mix, w_in, conv_a_w, conv_a_b, w_a_out, conv_b_w,
           conv_b_b, w_gate_r, b_gate_r, w_gate_i, b_gate_i, lru_lambda, w_b_out, w_o, g_pre_mlp,
           g_post_mlp, w_mlp_up, w_mlp_down):
    assert x.shape == (BATCH, SEQ, D) and c.shape == (BATCH, D)
    mod = _modulation(c, w_mod, b_mod)
    xt = jnp.transpose(x, (1, 0, 2)).reshape(SEQ * BATCH, D)
    row = lambda a: a.reshape(1, D)
    for l in range(DEPTH):
        w_gates = jnp.concatenate([w_gate_r[l], w_gate_i[l]], axis=-1).astype(bf16)
        xt = _mixer(xt, mod[l], row(g_pre_mix[l]), row(g_post_mix[l]), w_in[l].astype(bf16),
                    conv_a_w[l], row(conv_a_b[l]), w_a_out[l].astype(bf16), conv_b_w[l],
                    row(conv_b_b[l]), w_gates, row(b_gate_r[l]), row(b_gate_i[l]),
                    row(lru_lambda[l]), w_b_out[l].astype(bf16), w_o[l].astype(bf16))
        xt = _mlp(xt, mod[l], row(g_pre_mlp[l]), row(g_post_mlp[l]),
                  w_mlp_up[l].astype(bf16), w_mlp_down[l].astype(bf16))
    return jnp.transpose(xt.reshape(SEQ, BATCH, D), (1, 0, 2))
```
